```python
import jax, jax.numpy as jnp
from jax import lax
import numpy as np

D_MODEL = 1024
BATCH = 4
SEQ = 8192
DEPTH = 4

MLSTM_HEADS = 4
MLSTM_HEAD_DIM = D_MODEL // MLSTM_HEADS
MLSTM_WIDTH = MLSTM_HEADS * MLSTM_HEAD_DIM
MLSTM_CHUNK = 64
QK_CONV_WIDTH = 4
CONV_WIDTH = D_MODEL
SHORT_CONV_K = 3
D_FF = 3584
N_EXPERTS = 8
TOP_K = 2
N_DENSE = (DEPTH + 1) // 2
N_MOE = DEPTH // 2
ALPHA = (2 * DEPTH) ** 0.25
BETA = (8 * DEPTH) ** -0.25
LN_EPS = 1e-5
SPLIT_SIZES = (2 * MLSTM_WIDTH, MLSTM_WIDTH, MLSTM_WIDTH, MLSTM_HEADS, MLSTM_HEADS,
               CONV_WIDTH, CONV_WIDTH, CONV_WIDTH, D_MODEL, D_MODEL)
PROJ_WIDTH = sum(SPLIT_SIZES)

kernel_name = 'hybrid_mlstm_shortconv_moe_deepnorm'


def _split_points():
    pts, acc = [], 0
    for s in SPLIT_SIZES[:-1]:
        acc += s
        pts.append(acc)
    return pts


def layer_norm(x, g, b):
    xf = x.astype(jnp.float32)
    mu = xf.mean(-1, keepdims=True)
    var = jnp.square(xf - mu).mean(-1, keepdims=True)
    y = (xf - mu) * lax.rsqrt(var + LN_EPS) * g.astype(jnp.float32) + b.astype(jnp.float32)
    return y.astype(x.dtype)


def causal_depthwise_conv(u, w):
    K = w.shape[0]
    return lax.conv_general_dilated(
        u, w[:, None, :].astype(u.dtype), window_strides=(1,), padding=[(K - 1, 0)],
        dimension_numbers=('NWC', 'WIO', 'NWC'), feature_group_count=u.shape[-1])


def mlstm_chunkwise(q, k, v, i_pre, f_pre):
    Bsz, H, S, d = q.shape
    L = MLSTM_CHUNK
    nc = S // L
    f32 = jnp.float32
    lf = jax.nn.log_sigmoid(f_pre.astype(f32))
    li = i_pre.astype(f32)

    def chunks(t):
        t = t.astype(f32).reshape((Bsz, H, nc, L) + t.shape[3:])
        return jnp.moveaxis(t, 2, 0)

    qc, kc, vc, lic = chunks(q), chunks(k), chunks(v), chunks(li)
    bc = jnp.cumsum(chunks(lf), axis=-1)
    mask = jnp.tril(jnp.ones((L, L), dtype=bool))

    def step(carry, xs):
        C, n, m = carry
        qj, kj, vj, bj, lij = xs
        a = bj + m[..., None]
        Dm = bj[..., :, None] - bj[..., None, :] + lij[..., None, :]
        Dm = jnp.where(mask, Dm, -jnp.inf)
        mt = jnp.maximum(a, Dm.max(-1))
        W = jnp.exp(Dm - mt[..., None])
        scores = jnp.einsum('bhtk,bhsk->bhts', qj, kj) * W
        inter = jnp.exp(a - mt)
        num = jnp.einsum('bhts,bhsv->bhtv', scores, vj) \
            + inter[..., None] * jnp.einsum('bhtk,bhkv->bhtv', qj, C)
        den = scores.sum(-1) + inter * jnp.einsum('bhtk,bhk->bht', qj, n)
        h = num / jnp.maximum(jnp.abs(den), jnp.exp(-mt))[..., None]
        bL = bj[..., -1]
        g = bL[..., None] - bj + lij
        m_new = jnp.maximum(bL + m, g.max(-1))
        decay = jnp.exp(bL + m - m_new)
        ws = jnp.exp(g - m_new[..., None])
        kw = kj * ws[..., None]
        C_new = decay[..., None, None] * C + jnp.einsum('bhsk,bhsv->bhkv', kw, vj)
        n_new = decay[..., None] * n + kw.sum(-2)
        return (C_new, n_new, m_new), h

    init = (jnp.zeros((Bsz, H, d, d), f32), jnp.zeros((Bsz, H, d), f32), jnp.zeros((Bsz, H), f32))
    _, hs = lax.scan(step, init, (qc, kc, vc, bc, lic))
    return jnp.moveaxis(hs, 0, 2).reshape(Bsz, H, S, d)


def token_mixer(h, w_in, b_gates, conv_qk, hn_gain, conv_short, w_out):
    Bsz, S, _ = h.shape
    H, Dh, W = MLSTM_HEADS, MLSTM_HEAD_DIM, MLSTM_WIDTH
    p = h @ w_in
    qk, v, o, ig, fg, cb, cc, cx, ga, gb = jnp.split(p, _split_points(), axis=-1)
    qk = jax.nn.silu(causal_depthwise_conv(qk, conv_qk))
    q, k = jnp.split(qk, 2, axis=-1)
    heads = lambda t: t.reshape(Bsz, S, H, Dh).transpose(0, 2, 1, 3)
    i_pre = (ig + b_gates[:H]).transpose(0, 2, 1)
    f_pre = (fg + b_gates[H:]).transpose(0, 2, 1)
    hm = mlstm_chunkwise(heads(q), heads(k) * (Dh ** -0.5), heads(v), i_pre, f_pre)
    hm = hm.transpose(0, 2, 1, 3)
    mu = hm.mean(-1, keepdims=True)
    var = jnp.square(hm - mu).mean(-1, keepdims=True)
    hm = (hm - mu) * lax.rsqrt(var + LN_EPS) * hn_gain.reshape(H, Dh).astype(jnp.float32)
    y_a = jax.nn.sigmoid(o) * hm.reshape(Bsz, S, W).astype(h.dtype)
    y_b = cb * causal_depthwise_conv(cc * cx, conv_short)
    merged = jax.nn.sigmoid(ga) * y_a + jax.nn.sigmoid(gb) * y_b
    return merged @ w_out


def swiglu(h, w13, w2):
    a, b = jnp.split(h @ w13, 2, axis=-1)
    return (jax.nn.silu(a) * b) @ w2


def moe_ffn(h, w_router, w13, w2):
    Bsz, S, D = h.shape
    t = h.reshape(Bsz * S, D)
    logits = (t @ w_router).astype(jnp.float32)
    top_v, top_i = lax.top_k(logits, TOP_K)
    probs = jax.nn.softmax(top_v, axis=-1)
    gates = (jax.nn.one_hot(top_i, N_EXPERTS, dtype=jnp.float32) * probs[..., None]).sum(-2)
    out = jnp.zeros_like(t)
    for e in range(N_EXPERTS):
        out = out + gates[:, e, None].astype(t.dtype) * swiglu(t, w13[e], w2[e])
    return out.reshape(Bsz, S, D)


def setup_inputs(seed: int = 0) -> dict:
    key = jax.random.key(seed)
    ks = jax.random.split(key, 20)
    f32 = jnp.float32
    nrm = lambda k, shape, scale: jax.random.normal(k, shape, f32) * scale
    D, H, W, Wc = D_MODEL, MLSTM_HEADS, MLSTM_WIDTH, CONV_WIDTH
    b_gates = jnp.concatenate(
        [nrm(ks[5], (DEPTH, H), 0.1),
         jnp.linspace(3.0, 6.0, H, dtype=f32)[None, :] + nrm(ks[6], (DEPTH, H), 0.1)], axis=-1)
    return {
        'x': nrm(ks[0], (BATCH, SEQ, D), 1.0),
        'c': nrm(ks[1], (BATCH, D), 1.0),
        'w_ada': nrm(ks[2], (DEPTH, D, 6 * D), 0.5 * D ** -0.5),
        'b_ada': nrm(ks[3], (DEPTH, 6 * D), 0.02),
        'w_in': nrm(ks[4], (DEPTH, D, PROJ_WIDTH), D ** -0.5),
        'b_gates': b_gates,
        'conv_qk': nrm(ks[7], (DEPTH, QK_CONV_WIDTH, 2 * W), QK_CONV_WIDTH ** -0.5),
        'hn_gain': 1.0 + nrm(ks[8], (DEPTH, W), 0.05),
        'conv_short': nrm(ks[9], (DEPTH, SHORT_CONV_K, Wc), SHORT_CONV_K ** -0.5),
        'w_out': nrm(ks[10], (DEPTH, D, D), BETA * D ** -0.5),
        'ln_g': 1.0 + nrm(ks[11], (DEPTH, 2, D), 0.05),
        'ln_b': nrm(ks[12], (DEPTH, 2, D), 0.02),
        'dense_w13': nrm(ks[13], (N_DENSE, D, 2 * D_FF), D ** -0.5),
        'dense_w2': nrm(ks[14], (N_DENSE, D_FF, D), BETA * D_FF ** -0.5),
        'w_router': nrm(ks[15], (N_MOE, D, N_EXPERTS), D ** -0.5),
        'moe_w13': nrm(ks[16], (N_MOE, N_EXPERTS, D, 2 * D_FF), D ** -0.5),
        'moe_w2': nrm(ks[17], (N_MOE, N_EXPERTS, D_FF, D), BETA * D_FF ** -0.5),
    }


def reference(x, c, w_ada, b_ada, w_in, b_gates, conv_qk, hn_gain, conv_short, w_out,
              ln_g, ln_b, dense_w13, dense_w2, w_router, moe_w13, moe_w2):
    cond = jax.nn.silu(c)
    for l in range(DEPTH):
        mod = (cond @ w_ada[l] + b_ada[l])[:, None, :]
        sh_a, sc_a, g_a, sh_f, sc_f, g_f = jnp.split(mod, 6, axis=-1)
        h = x * (1 + sc_a) + sh_a
        y = token_mixer(h, w_in[l], b_gates[l], conv_qk[l], hn_gain[l], conv_short[l], w_out[l])
        x = layer_norm(ALPHA * x + (1 + g_a) * y, ln_g[l, 0], ln_b[l, 0])
        h = x * (1 + sc_f) + sh_f
        if l % 2 == 0:
            y = swiglu(h, dense_w13[l // 2], dense_w2[l // 2])
        else:
            y = moe_ffn(h, w_router[l // 2], moe_w13[l // 2], moe_w2[l // 2])
        x = layer_norm(ALPHA * x + (1 + g_f) * y, ln_g[l, 1], ln_b[l, 1])
    return x
```

```python
import functools

import jax
import jax.numpy as jnp
from jax import lax
from jax.experimental import pallas as pl
from jax.experimental.pallas import tpu as pltpu

F32 = jnp.float32
BF16 = jnp.bfloat16
I32 = jnp.int32
HIGHEST = lax.Precision.HIGHEST

HEADS = 4
TOP_K = 2
LN_EPS = 1e-5
LANES = 128
SUBLANES = 8
VMEM_LIMIT = 56 * 1024 * 1024
NEG_BIG = -1e30


def _params(sem):
    return pltpu.CompilerParams(dimension_semantics=sem, vmem_limit_bytes=VMEM_LIMIT)


def _pick(n, prefs):
    for p in prefs:
        if n % p == 0:
            return p
    return n


def _ln(z, g, b):
    mu = jnp.mean(z, axis=-1, keepdims=True)
    zc = z - mu
    var = jnp.mean(zc * zc, axis=-1, keepdims=True)
    return zc * lax.rsqrt(var + LN_EPS) * g + b


def _sigmoid(x):
    return 1.0 / (1.0 + jnp.exp(-x))


def _silu(x):
    return x * _sigmoid(x)


def _ada_kernel(c_ref, w_ref, b_ref, o_ref):
    c = c_ref[...]
    cond = _silu(c)
    o_ref[...] = jnp.dot(cond, w_ref[...], precision=HIGHEST,
                         preferred_element_type=F32) + b_ref[...]


def _ada_mod(c_pad, w_ada, b_ada):
    depth, d, d6 = w_ada.shape
    bp = c_pad.shape[0]
    nk = d6 // d
    return pl.pallas_call(
        _ada_kernel,
        grid=(depth, nk),
        in_specs=[
            pl.BlockSpec((bp, d), lambda l, k: (0, 0)),
            pl.BlockSpec((None, d, d), lambda l, k: (l, 0, k)),
            pl.BlockSpec((None, 1, d), lambda l, k: (l, 0, k)),
        ],
        out_specs=pl.BlockSpec((None, None, bp, d), lambda l, k: (l, k, 0, 0)),
        out_shape=jax.ShapeDtypeStruct((depth, nk, bp, d), F32),
        compiler_params=_params(("arbitrary", "arbitrary")),
        name="ada_mod",
    )(c_pad, w_ada, b_ada.reshape(depth, 1, d6))


def _inproj_kernel(x_ref, sc_ref, sh_ref, w_ref, wg_ref, p_ref, g_ref, hb_ref):
    @pl.when(pl.program_id(1) == 0)
    def _():
        h = x_ref[...] * (1.0 + sc_ref[...]) + sh_ref[...]
        hb_ref[...] = h.astype(BF16)
        g_ref[...] = jnp.dot(h, wg_ref[...], precision=HIGHEST, preferred_element_type=F32)

    p_ref[...] = jnp.dot(hb_ref[...], w_ref[...], preferred_element_type=F32).astype(BF16)


def _inproj(x2d, mod4, w_main, w_gates, seq):
    t, d = x2d.shape
    n = w_main.shape[1]
    tm = _pick(seq, (1024, 512, 256, 128))
    tn = _pick(n, (1024, 512, 256, 128))
    per_b = seq // tm
    return pl.pallas_call(
        _inproj_kernel,
        grid=(t // tm, n // tn),
        in_specs=[
            pl.BlockSpec((tm, d), lambda i, j: (i, 0)),
            pl.BlockSpec((None, None, 1, d), lambda i, j: (1, i // per_b, 0, 0)),
            pl.BlockSpec((None, None, 1, d), lambda i, j: (0, i // per_b, 0, 0)),
            pl.BlockSpec((d, tn), lambda i, j: (0, j)),
            pl.BlockSpec((d, LANES), lambda i, j: (0, 0)),
        ],
        out_specs=[
            pl.BlockSpec((tm, tn), lambda i, j: (i, j)),
            pl.BlockSpec((tm, LANES), lambda i, j: (i, 0)),
        ],
        out_shape=[jax.ShapeDtypeStruct((t, n), BF16), jax.ShapeDtypeStruct((t, LANES), F32)],
        scratch_shapes=[pltpu.VMEM((tm, d), BF16)],
        compiler_params=_params(("arbitrary", "arbitrary")),
        name="inproj",
    )(x2d, mod4, mod4, w_main, w_gates)


def _causal_conv(u, prev, w):
    k = w.shape[0]
    full = u * w[k - 1:k, :]
    ext = jnp.concatenate([prev, u[0:SUBLANES, :]], axis=0)
    head = ext * w[k - 1:k, :]
    for j in range(1, k):
        wj = w[k - 1 - j:k - j, :]
        full = full + pltpu.roll(u, j, 0) * wj
        head = head + pltpu.roll(ext, j, 0) * wj
    return jnp.concatenate([head[SUBLANES:2 * SUBLANES, :], full[SUBLANES:, :]], axis=0)


def _mlstm_head(q, k, vb, li_col, li_row, b_col, b_row, causal, c_ref, n_ref, m_ref):
    lc = q.shape[0]
    m_prev = m_ref[0:1, 0:1]
    a = b_col + m_prev
    dm = jnp.where(causal, b_col - b_row + li_row, NEG_BIG)
    mt = jnp.maximum(a, jnp.max(dm, axis=-1, keepdims=True))
    w = jnp.exp(dm - mt)
    qb = q.astype(BF16)
    kb = k.astype(BF16)
    s = lax.dot_general(qb, kb, (((1,), (1,)), ((), ())), preferred_element_type=F32) * w
    inter = jnp.exp(a - mt)
    cb = c_ref[...].astype(BF16)
    num = (jnp.dot(s.astype(BF16), vb, preferred_element_type=F32)
           + inter * jnp.dot(qb, cb, preferred_element_type=F32))
    n_row = n_ref[0:1, :]
    den = jnp.sum(s, axis=-1, keepdims=True) + inter * jnp.sum(q * n_row, axis=-1, keepdims=True)
    hout = num / jnp.maximum(jnp.abs(den), jnp.exp(-mt))
    b_last = b_col[lc - 1:lc, :]
    g_row = b_last - b_row + li_row
    m_new = jnp.maximum(b_last + m_prev, jnp.max(g_row, axis=-1, keepdims=True))
    decay = jnp.exp(b_last + m_prev - m_new)
    ws = jnp.exp(b_last - b_col + li_col - m_new)
    kw = k * ws
    c_ref[...] = decay * c_ref[...] + lax.dot_general(
        kw.astype(BF16), vb, (((0,), (0,)), ((), ())), preferred_element_type=F32)
    n_ref[0:1, :] = decay * n_row + jnp.sum(kw, axis=0, keepdims=True)
    m_ref[...] = jnp.broadcast_to(m_new, m_ref.shape)
    return hout


def _mixer_kernel(alpha, p_ref, g_ref, x_ref, ga_ref, cqk_ref, bg_ref, gain_ref, cs_ref,
                  wo_ref, lng_ref, lnb_ref, o_ref,
                  c_ref, n_ref, m_ref, qkprev_ref, csprev_ref, ya_ref):
    lc, d = x_ref.shape
    dh = d // HEADS

    @pl.when(pl.program_id(1) == 0)
    def _():
        c_ref[...] = jnp.zeros_like(c_ref)
        n_ref[...] = jnp.zeros_like(n_ref)
        m_ref[...] = jnp.zeros_like(m_ref)
        qkprev_ref[...] = jnp.zeros_like(qkprev_ref)
        csprev_ref[...] = jnp.zeros_like(csprev_ref)

    qk_pre = p_ref[:, 0:2 * d].astype(F32)
    qk = _silu(_causal_conv(qk_pre, qkprev_ref[...], cqk_ref[...]))
    qkprev_ref[...] = qk_pre[lc - SUBLANES:lc, :]

    gb = g_ref[...] + bg_ref[...]
    lf = jnp.minimum(gb, 0.0) - jnp.log1p(jnp.exp(-jnp.abs(gb)))
    row = lax.broadcasted_iota(I32, (lc, lc), 0)
    col = lax.broadcasted_iota(I32, (lc, lc), 1)
    causal = row >= col
    tri = causal.astype(F32)
    bcum = jnp.dot(tri, lf, precision=HIGHEST, preferred_element_type=F32)
    gb_t = gb.T
    bcum_t = bcum.T
    scale = float(dh) ** -0.5

    for h in range(HEADS):
        q = qk[:, h * dh:(h + 1) * dh]
        k = qk[:, d + h * dh:d + (h + 1) * dh] * scale
        vb = p_ref[:, 2 * d + h * dh:2 * d + (h + 1) * dh]
        hout = _mlstm_head(
            q, k, vb,
            gb[:, h:h + 1], gb_t[h:h + 1, :],
            bcum[:, HEADS + h:HEADS + h + 1], bcum_t[HEADS + h:HEADS + h + 1, :],
            causal, c_ref.at[h], n_ref.at[h], m_ref.at[h])
        mu = jnp.mean(hout, axis=-1, keepdims=True)
        hc = hout - mu
        var = jnp.mean(hc * hc, axis=-1, keepdims=True)
        hn = hc * lax.rsqrt(var + LN_EPS) * gain_ref[:, h * dh:(h + 1) * dh]
        o_gate = p_ref[:, 3 * d + h * dh:3 * d + (h + 1) * dh].astype(F32)
        ya_ref[:, h * dh:(h + 1) * dh] = _sigmoid(o_gate) * hn

    cb = p_ref[:, 4 * d:5 * d].astype(F32)
    ccx = p_ref[:, 5 * d:6 * d].astype(F32) * p_ref[:, 6 * d:7 * d].astype(F32)
    yb = cb * _causal_conv(ccx, csprev_ref[...], cs_ref[...])
    csprev_ref[...] = ccx[lc - SUBLANES:lc, :]

    ga = p_ref[:, 7 * d:8 * d].astype(F32)
    gbm = p_ref[:, 8 * d:9 * d].astype(F32)
    merged = _sigmoid(ga) * ya_ref[...] + _sigmoid(gbm) * yb
    y = jnp.dot(merged.astype(BF16), wo_ref[...], preferred_element_type=F32)
    z = alpha * x_ref[...] + (1.0 + ga_ref[...]) * y
    o_ref[...] = _ln(z, lng_ref[...], lnb_ref[...])


def _mixer(p, g, x2d, mod4, conv_qk, bgate, gain, conv_short, w_out_b, ln_g, ln_b, seq, alpha):
    t, d = x2d.shape
    n = p.shape[1]
    lc = _pick(seq, (256, 128, 64, 32, 16, 8))
    ns = seq // lc
    batch = t // seq
    dh = d // HEADS
    full = lambda a: pl.BlockSpec(a.shape, lambda b, s: (0,) * a.ndim)
    return pl.pallas_call(
        functools.partial(_mixer_kernel, alpha),
        grid=(batch, ns),
        in_specs=[
            pl.BlockSpec((lc, n), lambda b, s: (b * ns + s, 0)),
            pl.BlockSpec((lc, LANES), lambda b, s: (b * ns + s, 0)),
            pl.BlockSpec((lc, d), lambda b, s: (b * ns + s, 0)),
            pl.BlockSpec((None, None, 1, d), lambda b, s: (2, b, 0, 0)),
            full(conv_qk), full(bgate), full(gain), full(conv_short), full(w_out_b),
            full(ln_g), full(ln_b),
        ],
        out_specs=pl.BlockSpec((lc, d), lambda b, s: (b * ns + s, 0)),
        out_shape=jax.ShapeDtypeStruct((t, d), F32),
        scratch_shapes=[
            pltpu.VMEM((HEADS, dh, dh), F32),
            pltpu.VMEM((HEADS, SUBLANES, dh), F32),
            pltpu.VMEM((HEADS, SUBLANES, LANES), F32),
            pltpu.VMEM((SUBLANES, 2 * d), F32),
            pltpu.VMEM((SUBLANES, d), F32),
            pltpu.VMEM((lc, d), F32),
        ],
        compiler_params=_params(("arbitrary", "arbitrary")),
        name="mixer",
    )(p, g, x2d, mod4, conv_qk, bgate, gain, conv_short, w_out_b, ln_g, ln_b)


def _ffn_kernel(alpha, x_ref, sc_ref, sh_ref, gt_ref, w1_ref, w3_ref, w2_ref, lng_ref, lnb_ref,
                o_ref, hb_ref, acc_ref):
    j = pl.program_id(1)

    @pl.when(j == 0)
    def _():
        h = x_ref[...] * (1.0 + sc_ref[...]) + sh_ref[...]
        hb_ref[...] = h.astype(BF16)
        acc_ref[...] = jnp.zeros_like(acc_ref)

    hb = hb_ref[...]
    a = jnp.dot(hb, w1_ref[...], preferred_element_type=F32)
    b = jnp.dot(hb, w3_ref[...], preferred_element_type=F32)
    act = (_silu(a) * b).astype(BF16)
    acc_ref[...] += jnp.dot(act, w2_ref[...], preferred_element_type=F32)

    @pl.when(j == pl.num_programs(1) - 1)
    def _():
        z = alpha * x_ref[...] + (1.0 + gt_ref[...]) * acc_ref[...]
        o_ref[...] = _ln(z, lng_ref[...], lnb_ref[...])


def _ffn(x2d, mod4, w13_b, w2_b, ln_g, ln_b, seq, alpha):
    t, d = x2d.shape
    dff = w2_b.shape[0]
    tm = _pick(seq, (1024, 512, 256, 128))
    tf = _pick(dff, (512, 256, 128))
    nf = dff // tf
    per_b = seq // tm
    modspec = lambda k: pl.BlockSpec((None, None, 1, d), lambda i, j: (k, i // per_b, 0, 0))
    return pl.pallas_call(
        functools.partial(_ffn_kernel, alpha),
        grid=(t // tm, nf),
        in_specs=[
            pl.BlockSpec((tm, d), lambda i, j: (i, 0)),
            modspec(4), modspec(3), modspec(5),
            pl.BlockSpec((d, tf), lambda i, j: (0, j)),
            pl.BlockSpec((d, tf), lambda i, j: (0, j + nf)),
            pl.BlockSpec((tf, d), lambda i, j: (j, 0)),
            pl.BlockSpec((1, d), lambda i, j: (0, 0)),
            pl.BlockSpec((1, d), lambda i, j: (0, 0)),
        ],
        out_specs=pl.BlockSpec((tm, d), lambda i, j: (i, 0)),
        out_shape=jax.ShapeDtypeStruct((t, d), F32),
        scratch_shapes=[pltpu.VMEM((tm, d), BF16), pltpu.VMEM((tm, d), F32)],
        compiler_params=_params(("arbitrary", "arbitrary")),
        name="ffn_dense",
    )(x2d, mod4, mod4, mod4, w13_b, w13_b, w2_b, ln_g, ln_b)


def _router_kernel(n_exp, x_ref, sc_ref, sh_ref, wr_ref, h_ref, mi_ref, mf_ref, cnt_ref, run_ref):
    i = pl.program_id(0)
    tr = x_ref.shape[0]

    @pl.when(i == 0)
    def _():
        run_ref[...] = jnp.zeros_like(run_ref)

    h = x_ref[...] * (1.0 + sc_ref[...]) + sh_ref[...]
    h_ref[...] = h
    logits = jnp.dot(h, wr_ref[...], precision=HIGHEST, preferred_element_type=F32)
    lane = lax.broadcasted_iota(I32, (tr, LANES), 1).astype(F32)
    logits = jnp.where(lane < n_exp, logits, NEG_BIG)
    v1 = jnp.max(logits, axis=-1, keepdims=True)
    e1 = jnp.min(jnp.where(logits == v1, lane, float(LANES)), axis=-1, keepdims=True)
    rest = jnp.where(lane == e1, NEG_BIG, logits)
    v2 = jnp.max(rest, axis=-1, keepdims=True)
    e2 = jnp.min(jnp.where(rest == v2, lane, float(LANES)), axis=-1, keepdims=True)
    ex = jnp.exp(v2 - v1)
    p1 = 1.0 / (1.0 + ex)
    p2 = ex * p1
    oh1 = (lane == e1).astype(F32)
    oh2 = (lane == e2).astype(F32)
    both = oh1 + oh2
    r = lax.broadcasted_iota(I32, (tr, tr), 0)
    c = lax.broadcasted_iota(I32, (tr, tr), 1)
    strict = (r > c).astype(BF16)
    before = jnp.dot(strict, both.astype(BF16), preferred_element_type=F32)
    base = before + run_ref[0:1, :]
    pos1 = jnp.sum(oh1 * base, axis=-1, keepdims=True)
    pos2 = jnp.sum(oh2 * base, axis=-1, keepdims=True)
    mi = jnp.where(lane == 0, e1,
                   jnp.where(lane == 1, e2,
                             jnp.where(lane == 2, pos1, jnp.where(lane == 3, pos2, 0.0))))
    mi_ref[...] = mi.astype(I32)
    mf_ref[...] = jnp.where(lane == 0, p1, jnp.where(lane == 1, p2, 0.0))
    run_new = run_ref[0:1, :] + jnp.sum(both, axis=0, keepdims=True)
    run_ref[...] = jnp.broadcast_to(run_new, run_ref.shape)
    cnt_ref[...] = jnp.broadcast_to(run_new, cnt_ref.shape).astype(I32)


def _router(x2d, mod4, wr_pad, n_exp, seq):
    t, d = x2d.shape
    tr = _pick(seq, (512, 256, 128))
    per_b = seq // tr
    modspec = lambda k: pl.BlockSpec((None, None, 1, d), lambda i: (k, i // per_b, 0, 0))
    return pl.pallas_call(
        functools.partial(_router_kernel, n_exp),
        grid=(t // tr,),
        in_specs=[
            pl.BlockSpec((tr, d), lambda i: (i, 0)),
            modspec(4), modspec(3),
            pl.BlockSpec((d, LANES), lambda i: (0, 0)),
        ],
        out_specs=[
            pl.BlockSpec((tr, d), lambda i: (i, 0)),
            pl.BlockSpec((tr, LANES), lambda i: (i, 0)),
            pl.BlockSpec((tr, LANES), lambda i: (i, 0)),
            pl.BlockSpec((SUBLANES, LANES), lambda i: (0, 0)),
        ],
        out_shape=[
            jax.ShapeDtypeStruct((t, d), F32),
            jax.ShapeDtypeStruct((t, LANES), I32),
            jax.ShapeDtypeStruct((t, LANES), F32),
            jax.ShapeDtypeStruct((SUBLANES, LANES), I32),
        ],
        scratch_shapes=[pltpu.VMEM((SUBLANES, LANES), F32)],
        compiler_params=_params(("arbitrary",)),
        name="moe_router",
    )(x2d, mod4, mod4, wr_pad)


def _scatter_kernel(tb, off_ref, idx_ref, h_hbm, hs_in, hs_hbm, sem):
    del hs_in
    base = pl.program_id(0) * tb

    def row_copy(tok, dest):
        return pltpu.make_async_copy(h_hbm.at[pl.ds(tok, 1)], hs_hbm.at[pl.ds(dest, 1)], sem)

    def issue(tk, carry):
        for k in range(TOP_K):
            dest = off_ref[idx_ref[k, tk]] + idx_ref[TOP_K + k, tk]
            row_copy(base + tk, dest).start()
        return carry

    lax.fori_loop(0, tb, issue, 0)

    def drain(tk, carry):
        for k in range(TOP_K):
            row_copy(0, 0).wait()
        return carry

    lax.fori_loop(0, tb, drain, 0)


def _scatter_rows(off, idx4, h, rows):
    t, d = h.shape
    tb = _pick(t, (512, 256, 128))
    hs0 = jnp.zeros((rows, d), F32)
    return pl.pallas_call(
        functools.partial(_scatter_kernel, tb),
        grid_spec=pltpu.PrefetchScalarGridSpec(
            num_scalar_prefetch=1,
            grid=(t // tb,),
            in_specs=[
                pl.BlockSpec((2 * TOP_K, tb), lambda i, off: (0, i), memory_space=pltpu.SMEM),
                pl.BlockSpec(memory_space=pl.ANY),
                pl.BlockSpec(memory_space=pl.ANY),
            ],
            out_specs=pl.BlockSpec(memory_space=pl.ANY),
            scratch_shapes=[pltpu.SemaphoreType.DMA(())],
        ),
        out_shape=jax.ShapeDtypeStruct((rows, d), F32),
        input_output_aliases={3: 0},
        compiler_params=_params(("arbitrary",)),
        name="moe_scatter",
    )(off, idx4, h, hs0)


def _gmm_kernel(te_ref, nu_ref, x_ref, w1_ref, w3_ref, w2_ref, o_ref, hb_ref, acc_ref):
    i = pl.program_id(0)
    j = pl.program_id(1)

    @pl.when(i < nu_ref[0])
    def _():
        @pl.when(j == 0)
        def _():
            hb_ref[...] = x_ref[...].astype(BF16)
            acc_ref[...] = jnp.zeros_like(acc_ref)

        hb = hb_ref[...]
        a = jnp.dot(hb, w1_ref[...], preferred_element_type=F32)
        b = jnp.dot(hb, w3_ref[...], preferred_element_type=F32)
        act = (_silu(a) * b).astype(BF16)
        acc_ref[...] += jnp.dot(act, w2_ref[...], preferred_element_type=F32)

        @pl.when(j == pl.num_programs(1) - 1)
        def _():
            o_ref[...] = acc_ref[...]

    @pl.when(jnp.logical_and(i >= nu_ref[0], j == 0))
    def _():
        o_ref[...] = jnp.zeros_like(o_ref)


def _gmm(tile_e, n_used, hs, w13_b, w2_b, tm):
    rows, d = hs.shape
    dff = w2_b.shape[1]
    tf = _pick(dff, (512, 256, 128))
    nf = dff // tf
    nt = rows // tm

    def row_map(i, j, te, nu):
        return (jnp.minimum(i, nu[0] - 1), 0)

    def jj(i, j, nu):
        return jnp.where(i < nu[0], j, nf - 1)

    return pl.pallas_call(
        _gmm_kernel,
        grid_spec=pltpu.PrefetchScalarGridSpec(
            num_scalar_prefetch=2,
            grid=(nt, nf),
            in_specs=[
                pl.BlockSpec((tm, d), row_map),
                pl.BlockSpec((None, d, tf), lambda i, j, te, nu: (te[i], 0, jj(i, j, nu))),
                pl.BlockSpec((None, d, tf), lambda i, j, te, nu: (te[i], 0, jj(i, j, nu) + nf)),
                pl.BlockSpec((None, tf, d), lambda i, j, te, nu: (te[i], jj(i, j, nu), 0)),
            ],
            out_specs=pl.BlockSpec((tm, d), lambda i, j, te, nu: (i, 0)),
            scratch_shapes=[pltpu.VMEM((tm, d), BF16), pltpu.VMEM((tm, d), F32)],
        ),
        out_shape=jax.ShapeDtypeStruct((rows, d), F32),
        compiler_params=_params(("arbitrary", "arbitrary")),
        name="moe_gmm",
    )(tile_e, n_used, hs, w13_b, w13_b, w2_b)


def _combine_kernel(alpha, tc, off_ref, idx_ref, ys_hbm, x_ref, mf_ref, gt_ref, lng_ref, lnb_ref,
                    o_ref, ybuf, sem):
    def row_copy(k, tk, src):
        return pltpu.make_async_copy(ys_hbm.at[pl.ds(src, 1)], ybuf.at[k, pl.ds(tk, 1)], sem)

    def issue(tk, carry):
        for k in range(TOP_K):
            src = off_ref[idx_ref[k, tk]] + idx_ref[TOP_K + k, tk]
            row_copy(k, tk, src).start()
        return carry

    lax.fori_loop(0, tc, issue, 0)

    def drain(tk, carry):
        for k in range(TOP_K):
            row_copy(k, 0, 0).wait()
        return carry

    lax.fori_loop(0, tc, drain, 0)

    mf = mf_ref[...]
    y = mf[:, 0:1] * ybuf[0] + mf[:, 1:2] * ybuf[1]
    z = alpha * x_ref[...] + (1.0 + gt_ref[...]) * y
    o_ref[...] = _ln(z, lng_ref[...], lnb_ref[...])


def _combine(off, idx4, ys, x2d, meta_f, mod4, ln_g, ln_b, seq, alpha):
    t, d = x2d.shape
    tc = _pick(seq, (256, 128))
    per_b = seq // tc
    return pl.pallas_call(
        functools.partial(_combine_kernel, alpha, tc),
        grid_spec=pltpu.PrefetchScalarGridSpec(
            num_scalar_prefetch=1,
            grid=(t // tc,),
            in_specs=[
                pl.BlockSpec((2 * TOP_K, tc), lambda i, off: (0, i), memory_space=pltpu.SMEM),
                pl.BlockSpec(memory_space=pl.ANY),
                pl.BlockSpec((tc, d), lambda i, off: (i, 0)),
                pl.BlockSpec((tc, LANES), lambda i, off: (i, 0)),
                pl.BlockSpec((None, None, 1, d), lambda i, off: (5, i // per_b, 0, 0)),
                pl.BlockSpec((1, d), lambda i, off: (0, 0)),
                pl.BlockSpec((1, d), lambda i, off: (0, 0)),
            ],
            out_specs=pl.BlockSpec((tc, d), lambda i, off: (i, 0)),
            scratch_shapes=[pltpu.VMEM((TOP_K, tc, d), F32), pltpu.SemaphoreType.DMA(())],
        ),
        out_shape=jax.ShapeDtypeStruct((t, d), F32),
        compiler_params=_params(("arbitrary",)),
        name="moe_combine",
    )(off, idx4, ys, x2d, meta_f, mod4, ln_g, ln_b)


def _moe(x2d, mod4, wr_pad, n_exp, w13_b, w2_b, ln_g, ln_b, seq, alpha):
    t, d = x2d.shape
    tm = _pick(seq, (512, 256, 128))
    h, meta_i, meta_f, counts = _router(x2d, mod4, wr_pad, n_exp, seq)
    cnt = counts[0, :n_exp]
    tiles = (cnt + tm - 1) // tm
    tile_end = jnp.cumsum(tiles)
    off = ((tile_end - tiles) * tm).astype(I32)
    nt = (TOP_K * t) // tm + n_exp
    n_used = tile_end[-1:].astype(I32)
    tile_i = jnp.minimum(jnp.arange(nt, dtype=I32), n_used - 1)
    tile_e = jnp.searchsorted(tile_end, tile_i, side="right").astype(I32)
    idx4 = meta_i[:, :2 * TOP_K].T
    hs = _scatter_rows(off, idx4, h, nt * tm)
    ys = _gmm(tile_e, n_used, hs, w13_b, w2_b, tm)
    return _combine(off, idx4, ys, x2d, meta_f, mod4, ln_g, ln_b, seq, alpha)


def kernel(x, c, w_ada, b_ada, w_in, b_gates, conv_qk, hn_gain, conv_short, w_out, ln_g, ln_b,
           dense_w13, dense_w2, w_router, moe_w13, moe_w2):
    batch, seq, d = x.shape
    depth = w_ada.shape[0]
    n_exp = w_router.shape[-1]
    alpha = float((2 * depth) ** 0.25)
    hh = HEADS

    bp = -(-batch // SUBLANES) * SUBLANES
    c_pad = jnp.pad(c, ((0, bp - batch), (0, 0)))
    mod = _ada_mod(c_pad, w_ada, b_ada)
    mod = mod.reshape(depth, 6, bp, 1, d)

    g0 = 4 * d
    x2d = x.reshape(batch * seq, d)
    for l in range(depth):
        wl = w_in[l]
        w_main = jnp.concatenate([wl[:, :g0], wl[:, g0 + 2 * hh:]], axis=1).astype(BF16)
        w_gates = jnp.pad(wl[:, g0:g0 + 2 * hh], ((0, 0), (0, LANES - 2 * hh)))
        bgate = jnp.pad(b_gates[l], (0, LANES - 2 * hh)).reshape(1, LANES)
        p, g = _inproj(x2d, mod[l], w_main, w_gates, seq)
        x2d = _mixer(p, g, x2d, mod[l], conv_qk[l], bgate, hn_gain[l].reshape(1, d),
                     conv_short[l], w_out[l].astype(BF16),
                     ln_g[l, 0].reshape(1, d), ln_b[l, 0].reshape(1, d), seq, alpha)
        lg = ln_g[l, 1].reshape(1, d)
        lb = ln_b[l, 1].reshape(1, d)
        if l % 2 == 0:
            x2d = _ffn(x2d, mod[l], dense_w13[l // 2].astype(BF16), dense_w2[l // 2].astype(BF16),
                       lg, lb, seq, alpha)
        else:
            wr_pad = jnp.pad(w_router[l // 2], ((0, 0), (0, LANES - n_exp)))
            x2d = _moe(x2d, mod[l], wr_pad, n_exp, moe_w13[l // 2].astype(BF16),
                       moe_w2[l // 2].astype(BF16), lg, lb, seq, alpha)
    return x2d.reshape(batch, seq, d)
```
